```python
import jax, jax.numpy as jnp
from jax import lax
import numpy as np

D_MODEL = 1024
BATCH = 1
SEQ = 16384
DEPTH = 1
DEC_BATCH = 128
DEC_SEQ = 8
PAST_LEN = 8192
PAGE_SIZE = 128

N_HEADS = 8
HEAD_DIM = 64
ATT_WIDTH = N_HEADS * HEAD_DIM
MOBA_BLOCK = 256
MOBA_TOPK = 3
Q_CHUNK = 128
ROPE_DIM = HEAD_DIM // 4
ROPE_THETA = 500000.0
POOL_WINDOWS = (2, 4, 8, 16)
POOL_GROUPS = len(POOL_WINDOWS)
POOL_WIDTH = D_MODEL // 2
POOL_GROUP_WIDTH = POOL_WIDTH // POOL_GROUPS
POOL_MAX = max(POOL_WINDOWS)
POOL_BUF = POOL_MAX - 1
MIX_WIDTH = ATT_WIDTH + POOL_WIDTH
IN_WIDTH = 3 * ATT_WIDTH + POOL_WIDTH
PEER_HEADS = 8
PEER_KEY_DIM = 256
PEER_N_KEYS = 128
PEER_EXPERTS = PEER_N_KEYS * PEER_N_KEYS
PEER_TOPK = 16
PEER_CHUNK = 256
NORM_EPS = 1e-6

kernel_name = 'hymba_moba_pool_peer_step'


def rms_norm(x, g):
    xf = x.astype(jnp.float32)
    y = xf * lax.rsqrt(jnp.mean(xf * xf, axis=-1, keepdims=True) + NORM_EPS)
    return (y * g.astype(jnp.float32)).astype(x.dtype)


def partial_rope(x, pos):
    half = ROPE_DIM // 2
    inv_freq = 1.0 / (ROPE_THETA ** (jnp.arange(half, dtype=jnp.float32) * (2.0 / ROPE_DIM)))
    ang = pos.astype(jnp.float32)[:, None] * inv_freq[None, :]
    cos = jnp.cos(ang)[:, None, :]
    sin = jnp.sin(ang)[:, None, :]
    xf = x.astype(jnp.float32)
    x1 = xf[..., :half]
    x2 = xf[..., half:ROPE_DIM]
    out = jnp.concatenate([x1 * cos - x2 * sin, x2 * cos + x1 * sin, xf[..., ROPE_DIM:]], axis=-1)
    return out.astype(x.dtype)


def mixer_inputs(x, norm_g, w_in, q_g, k_g, pos):
    h = rms_norm(x, norm_g)
    z = h @ w_in
    lead = x.shape[:-1]
    q = z[..., :ATT_WIDTH].reshape(*lead, N_HEADS, HEAD_DIM)
    k = z[..., ATT_WIDTH:2 * ATT_WIDTH].reshape(*lead, N_HEADS, HEAD_DIM)
    v = z[..., 2 * ATT_WIDTH:3 * ATT_WIDTH].reshape(*lead, N_HEADS, HEAD_DIM)
    p = z[..., 3 * ATT_WIDTH:]
    q = partial_rope(rms_norm(q, q_g), pos)
    k = partial_rope(rms_norm(k, k_g), pos)
    return q, k, v, p


def moba_attend(q, k_sel, v_sel, sel_valid, k_own, v_own, own_mask):
    s_own = jnp.einsum('bhqd,bhld->bhql', q, k_own).astype(jnp.float32)
    s_own = jnp.where(own_mask, s_own, -jnp.inf)
    if k_sel is None:
        p = jax.nn.softmax(s_own, axis=-1).astype(v_own.dtype)
        return jnp.einsum('bhql,bhld->bhqd', p, v_own)
    s_sel = jnp.einsum('bhqd,bhqmd->bhqm', q, k_sel).astype(jnp.float32)
    if sel_valid is not None:
        s_sel = jnp.where(sel_valid, s_sel, -jnp.inf)
    m = k_sel.shape[3]
    p = jax.nn.softmax(jnp.concatenate([s_sel, s_own], axis=-1), axis=-1).astype(v_own.dtype)
    return (jnp.einsum('bhqm,bhqmd->bhqd', p[..., :m], v_sel)
            + jnp.einsum('bhql,bhld->bhqd', p[..., m:], v_own))


def moba_prompt(q, k, v):
    B, S, H, hd = q.shape
    nb = -(-S // MOBA_BLOCK)
    s_pad = nb * MOBA_BLOCK
    n_qc = S // Q_CHUNK
    qh = (q * (HEAD_DIM ** -0.5)).transpose(0, 2, 1, 3)
    pad = ((0, 0), (0, 0), (0, s_pad - S), (0, 0))
    kh = jnp.pad(k.transpose(0, 2, 1, 3), pad)
    vh = jnp.pad(v.transpose(0, 2, 1, 3), pad)
    kb = kh.reshape(B, H, nb, MOBA_BLOCK, hd)
    vb = vh.reshape(B, H, nb, MOBA_BLOCK, hd)
    n_sel = min(MOBA_TOPK, nb - 1)
    qc = qh.reshape(B, H, n_qc, Q_CHUNK, hd).transpose(2, 0, 1, 3, 4)
    chunk_ids = jnp.arange(n_qc)
    bi = jnp.arange(B)[:, None, None, None]
    hi = jnp.arange(H)[None, :, None, None]
    if n_sel > 0:
        own_blk = jnp.arange(S) // MOBA_BLOCK
        kmean = kb.astype(jnp.float32).mean(axis=3)
        gate = jnp.einsum('bhsd,bhnd->bhsn', qh.astype(jnp.float32), kmean)
        fully_past = jnp.arange(nb)[None, :] < own_blk[:, None]
        gate = jnp.where(fully_past, gate, -jnp.inf)
        _, idx = lax.top_k(gate, n_sel)
        valid = idx < own_blk[:, None]
        idx_c = idx.reshape(B, H, n_qc, Q_CHUNK, n_sel).transpose(2, 0, 1, 3, 4)
        val_c = valid.reshape(B, H, n_qc, Q_CHUNK, n_sel).transpose(2, 0, 1, 3, 4)
        xs = (chunk_ids, qc, idx_c, val_c)
    else:
        xs = (chunk_ids, qc)

    def one_chunk(args):
        c, q_c = args[0], args[1]
        q0 = c * Q_CHUNK
        own_start = (q0 // MOBA_BLOCK) * MOBA_BLOCK
        k_own = lax.dynamic_slice_in_dim(kh, own_start, MOBA_BLOCK, axis=2)
        v_own = lax.dynamic_slice_in_dim(vh, own_start, MOBA_BLOCK, axis=2)
        own_mask = (own_start + jnp.arange(MOBA_BLOCK))[None, :] <= (q0 + jnp.arange(Q_CHUNK))[:, None]
        if n_sel == 0:
            return moba_attend(q_c, None, None, None, k_own, v_own, own_mask)
        idx_q, val_q = args[2], args[3]
        k_sel = kb[bi, hi, idx_q].reshape(B, H, Q_CHUNK, n_sel * MOBA_BLOCK, hd)
        v_sel = vb[bi, hi, idx_q].reshape(B, H, Q_CHUNK, n_sel * MOBA_BLOCK, hd)
        sel_valid = jnp.repeat(val_q, MOBA_BLOCK, axis=-1)
        return moba_attend(q_c, k_sel, v_sel, sel_valid, k_own, v_own, own_mask)

    out = lax.map(one_chunk, xs)
    out = out.transpose(1, 2, 0, 3, 4).reshape(B, H, S, hd)
    return out.transpose(0, 2, 1, 3).reshape(B, S, H * hd)


def cached_block_means(cache_k, layer, pages, n_blocks):
    rows = cache_k[layer, pages].astype(jnp.float32)
    db, _, _, h, hd = rows.shape
    return rows.reshape(db, n_blocks, MOBA_BLOCK, h, hd).mean(axis=2)


def moba_sample(q, k_new, v_new, cache_k, cache_v, layer, page_table):
    DB, DS, H, hd = q.shape
    page = cache_k.shape[2]
    past = page_table.shape[1] * page
    ppb = MOBA_BLOCK // page
    own_blk = past // MOBA_BLOCK
    assert MOBA_BLOCK % page == 0 and past % MOBA_BLOCK + DS <= MOBA_BLOCK
    r = past - own_blk * MOBA_BLOCK
    qh = (q * (HEAD_DIM ** -0.5)).transpose(0, 2, 1, 3)
    own_pages = page_table[:, own_blk * ppb:]
    k_own = jnp.concatenate([cache_k[layer, own_pages].reshape(DB, r, H, hd), k_new], axis=1).transpose(0, 2, 1, 3)
    v_own = jnp.concatenate([cache_v[layer, own_pages].reshape(DB, r, H, hd), v_new], axis=1).transpose(0, 2, 1, 3)
    L = r + DS
    own_mask = jnp.arange(L)[None, :] - r <= jnp.arange(DS)[:, None]
    n_sel = min(MOBA_TOPK, own_blk)
    if n_sel == 0:
        out = moba_attend(qh, None, None, None, k_own, v_own, own_mask)
    else:
        kmean = cached_block_means(cache_k, layer, page_table[:, :own_blk * ppb], own_blk)
        gate = jnp.einsum('bhqd,bnhd->bhqn', qh.astype(jnp.float32), kmean)
        _, idx = lax.top_k(gate, n_sel)
        bi = jnp.arange(DB)[:, None, None, None]
        hi = jnp.arange(H)[None, :, None, None]
        sub = jnp.arange(ppb)

        def one_token(args):
            q_t, idx_t, m_t = args
            phys = page_table[bi, idx_t[..., None] * ppb + sub]
            k_sel = cache_k[layer, phys, :, hi, :].reshape(DB, H, 1, n_sel * MOBA_BLOCK, hd)
            v_sel = cache_v[layer, phys, :, hi, :].reshape(DB, H, 1, n_sel * MOBA_BLOCK, hd)
            return moba_attend(q_t[:, :, None], k_sel, v_sel, None, k_own, v_own, m_t[None])[:, :, 0]

        out = lax.map(one_token, (qh.transpose(2, 0, 1, 3), idx.transpose(2, 0, 1, 3), own_mask))
        out = out.transpose(1, 2, 0, 3)
    return out.transpose(0, 2, 1, 3).reshape(DB, DS, H * hd)


def pool_mix(buf, pos_out, n_out, w_pool, pool_scale):
    B, L, P = buf.shape
    cs = jnp.cumsum(buf.astype(jnp.float32), axis=1)
    cz = jnp.concatenate([jnp.zeros((B, POOL_MAX, P), jnp.float32), cs], axis=1)
    top = cz[:, L - n_out + POOL_MAX:]
    own = buf[:, L - n_out:].astype(jnp.float32)
    outs = []
    for g, w in enumerate(POOL_WINDOWS):
        sl = slice(g * POOL_GROUP_WIDTH, (g + 1) * POOL_GROUP_WIDTH)
        low = cz[:, L - n_out + POOL_MAX - w:L + POOL_MAX - w, sl]
        cnt = jnp.minimum(w, pos_out + 1).astype(jnp.float32)[None, :, None]
        z = ((top[..., sl] - low) / cnt - own[..., sl]).astype(buf.dtype)
        outs.append(jnp.einsum('bsc,cd->bsd', z, w_pool[g]))
    return jnp.concatenate(outs, axis=-1) * pool_scale


def peer_ffn(h, wq, subkeys, u_tab, v_tab):
    T, D = h.shape
    n_chunks = -(-T // PEER_CHUNK)
    hp = jnp.pad(h, ((0, n_chunks * PEER_CHUNK - T), (0, 0))).reshape(n_chunks, PEER_CHUNK, D)
    half = PEER_KEY_DIM // 2
    n_cand = PEER_TOPK * PEER_TOPK

    def one_chunk(xc):
        q = (xc @ wq).reshape(PEER_CHUNK, PEER_HEADS, PEER_KEY_DIM)
        s1 = jnp.einsum('chd,hnd->chn', q[..., :half], subkeys[:, 0]).astype(jnp.float32)
        s2 = jnp.einsum('chd,hnd->chn', q[..., half:], subkeys[:, 1]).astype(jnp.float32)
        v1, i1 = lax.top_k(s1, PEER_TOPK)
        v2, i2 = lax.top_k(s2, PEER_TOPK)
        cand = (v1[..., :, None] + v2[..., None, :]).reshape(PEER_CHUNK, PEER_HEADS, n_cand)
        cand_id = (i1[..., :, None] * PEER_N_KEYS + i2[..., None, :]).reshape(PEER_CHUNK, PEER_HEADS, n_cand)
        score, pick = lax.top_k(cand, PEER_TOPK)
        eid = jnp.take_along_axis(cand_id, pick, axis=-1)
        gate = jax.nn.softmax(score, axis=-1).astype(xc.dtype)
        act = jax.nn.gelu(jnp.einsum('cd,chkd->chk', xc, u_tab[eid]), approximate=False)
        return jnp.einsum('chk,chkd->cd', gate * act, v_tab[eid])

    return lax.map(one_chunk, hp).reshape(n_chunks * PEER_CHUNK, D)[:T]


def mix_out_and_ffn(x, attn, pool, w_out, norm2_g, peer_wq, peer_subkeys, peer_u, peer_v):
    x = x + jnp.concatenate([attn, pool.astype(attn.dtype)], axis=-1) @ w_out
    h = rms_norm(x, norm2_g)
    d = x.shape[-1]
    return x + peer_ffn(h.reshape(-1, d), peer_wq, peer_subkeys, peer_u, peer_v).reshape(x.shape)


def setup_inputs(seed: int = 0) -> dict:
    key = jax.random.key(seed)
    ks = jax.random.split(key, 18)
    f32 = jnp.float32
    n_pages = PAST_LEN // PAGE_SIZE
    n_used = DEC_BATCH * n_pages
    n_pool = (5 * n_used + 3) // 4

    def nrm(k, shape, scale=1.0):
        return jax.random.normal(k, shape, f32) * scale

    page_table = jax.random.permutation(ks[5], n_pool)[:n_used].reshape(DEC_BATCH, n_pages).astype(jnp.int32)
    return {
        'x_prompt': nrm(ks[0], (BATCH, SEQ, D_MODEL)),
        'x_sample': nrm(ks[1], (DEC_BATCH, DEC_SEQ, D_MODEL)),
        'cache_k': nrm(ks[2], (DEPTH, n_pool, PAGE_SIZE, N_HEADS, HEAD_DIM)),
        'cache_v': nrm(ks[3], (DEPTH, n_pool, PAGE_SIZE, N_HEADS, HEAD_DIM)),
        'state_pool_rows': nrm(ks[4], (DEPTH, DEC_BATCH, POOL_BUF, POOL_WIDTH)),
        'page_table': page_table,
        'norm1_g': 1.0 + nrm(ks[6], (DEPTH, D_MODEL), 0.02),
        'w_in': nrm(ks[7], (DEPTH, D_MODEL, IN_WIDTH), D_MODEL ** -0.5),
        'q_norm_g': 1.0 + nrm(ks[8], (DEPTH, HEAD_DIM), 0.02),
        'k_norm_g': 1.0 + nrm(ks[9], (DEPTH, HEAD_DIM), 0.02),
        'w_pool': nrm(ks[10], (DEPTH, POOL_GROUPS, POOL_GROUP_WIDTH, POOL_GROUP_WIDTH), POOL_GROUP_WIDTH ** -0.5),
        'pool_scale': 1.0 + nrm(ks[11], (DEPTH, POOL_WIDTH), 0.02),
        'w_out': nrm(ks[12], (DEPTH, MIX_WIDTH, D_MODEL), MIX_WIDTH ** -0.5),
        'norm2_g': 1.0 + nrm(ks[13], (DEPTH, D_MODEL), 0.02),
        'peer_wq': nrm(ks[14], (DEPTH, D_MODEL, PEER_HEADS * PEER_KEY_DIM), D_MODEL ** -0.5),
        'peer_subkeys': nrm(ks[15], (DEPTH, PEER_HEADS, 2, PEER_N_KEYS, PEER_KEY_DIM // 2), (PEER_KEY_DIM // 2) ** -0.5),
        'peer_u': nrm(ks[16], (DEPTH, PEER_EXPERTS, D_MODEL), D_MODEL ** -0.5),
        'peer_v': nrm(ks[17], (DEPTH, PEER_EXPERTS, D_MODEL), PEER_HEADS ** -0.5),
    }


def reference(x_prompt, x_sample, cache_k, cache_v, state_pool_rows, page_table,
              norm1_g, w_in, q_norm_g, k_norm_g, w_pool, pool_scale, w_out, norm2_g,
              peer_wq, peer_subkeys, peer_u, peer_v):
    B, S, _ = x_prompt.shape
    DB, DS, _ = x_sample.shape
    past = page_table.shape[1] * cache_k.shape[2]
    pos_p = jnp.arange(S)
    pos_s = past + jnp.arange(DS)
    yp, ys = x_prompt, x_sample
    kp_l, vp_l, pp_l, ks_l, vs_l, ps_l = [], [], [], [], [], []
    for l in range(DEPTH):
        q, k, v, p = mixer_inputs(yp, norm1_g[l], w_in[l], q_norm_g[l], k_norm_g[l], pos_p)
        attn = moba_prompt(q, k, v)
        pool = pool_mix(p, pos_p, S, w_pool[l], pool_scale[l])
        yp = mix_out_and_ffn(yp, attn, pool, w_out[l], norm2_g[l], peer_wq[l], peer_subkeys[l], peer_u[l], peer_v[l])
        kp_l.append(k)
        vp_l.append(v)
        pp_l.append(p[:, S - POOL_BUF:])
        q, k, v, p = mixer_inputs(ys, norm1_g[l], w_in[l], q_norm_g[l], k_norm_g[l], pos_s)
        attn = moba_sample(q, k, v, cache_k, cache_v, l, page_table)
        buf = jnp.concatenate([state_pool_rows[l], p], axis=1)
        pool = pool_mix(buf, pos_s, DS, w_pool[l], pool_scale[l])
        ys = mix_out_and_ffn(ys, attn, pool, w_out[l], norm2_g[l], peer_wq[l], peer_subkeys[l], peer_u[l], peer_v[l])
        ks_l.append(k)
        vs_l.append(v)
        ps_l.append(buf[:, buf.shape[1] - POOL_BUF:])
    return (yp, ys, jnp.stack(kp_l), jnp.stack(vp_l), jnp.stack(pp_l), jnp.stack(ks_l), jnp.stack(vs_l), jnp.stack(ps_l))
```

```python
import functools
import math

import jax
import jax.numpy as jnp
import numpy as np
from jax import lax
from jax.experimental import pallas as pl
from jax.experimental.pallas import tpu as pltpu

F32 = jnp.float32
BF16 = jnp.bfloat16

N_HEADS = 8
HEAD_DIM = 64
ATT_WIDTH = N_HEADS * HEAD_DIM
MOBA_BLOCK = 256
MOBA_TOPK = 3
ROPE_DIM = HEAD_DIM // 4
ROPE_THETA = 500000.0
POOL_WINDOWS = (2, 4, 8, 16)
POOL_GROUP_WIDTH = 128
POOL_WIDTH = POOL_GROUP_WIDTH * len(POOL_WINDOWS)
POOL_MAX = max(POOL_WINDOWS)
POOL_BUF = POOL_MAX - 1
PEER_HEADS = 8
PEER_HALF = 128
PEER_N_KEYS = 128
PEER_TOPK = 16
NORM_EPS = 1e-6

LANES = 128
SUBLANES = 8
MASK_NEG = -1e30
VMEM_LIMIT = 56 * 1024 * 1024

ROW_TILE = 512
MID_TILE = 256
PEER_TOK_TILE = 512
PEER_EXP_TILE = 512
SAMPLE_CHUNK_PAGES = 8


def _dot(a, b):
    return jnp.dot(a, b, preferred_element_type=F32)


def _dot_nt(a, b):
    return lax.dot_general(a, b, (((1,), (1,)), ((), ())), preferred_element_type=F32)


def _rms(x, g):
    ms = jnp.mean(x * x, axis=-1, keepdims=True)
    return x * lax.rsqrt(ms + NORM_EPS) * g


def _head_rms(x, ones_bd, g_tiled):
    sq = x * x
    hi = sq.astype(BF16)
    lo = (sq - hi.astype(F32)).astype(BF16)
    ss = _dot(hi, ones_bd) + _dot(lo, ones_bd)
    return x * lax.rsqrt(ss * (1.0 / HEAD_DIM) + NORM_EPS) * g_tiled


def _rope(x, c, sa, sb):
    reps = x.shape[1] // LANES
    c4 = jnp.concatenate([c] * reps, axis=1)
    sa4 = jnp.concatenate([sa] * reps, axis=1)
    sb4 = jnp.concatenate([sb] * reps, axis=1)
    half = ROPE_DIM // 2
    return x * c4 + pltpu.roll(x, x.shape[1] - half, 1) * sa4 + pltpu.roll(x, half, 1) * sb4


def _qkvp(x_ref, g1_ref, win_ref, qg_ref, kg_ref, ones_ref, c_ref, sa_ref, sb_ref):
    h = _rms(x_ref[...], g1_ref[...]).astype(BF16)
    z = _dot(h, win_ref[...])
    a = ATT_WIDTH
    q, k, v, p = z[:, :a], z[:, a:2 * a], z[:, 2 * a:3 * a], z[:, 3 * a:]
    ones_bd = ones_ref[...]
    c, sa, sb = c_ref[...], sa_ref[...], sb_ref[...]
    q = _rope(_head_rms(q, ones_bd, qg_ref[...]), c, sa, sb) * (HEAD_DIM ** -0.5)
    k = _rope(_head_rms(k, ones_bd, kg_ref[...]), c, sa, sb)
    return q, k, v, p


def _pool_project(zs, wpool_ref, pscale_ref):
    outs = [_dot(z.astype(BF16), wpool_ref[g]) for g, z in enumerate(zs)]
    return jnp.concatenate(outs, axis=1) * pscale_ref[...]


def _inproj_prompt_kernel(x_ref, g1_ref, win_ref, qg_ref, kg_ref, ones_ref, c_ref, sa_ref, sb_ref,
                          wpool_ref, pscale_ref,
                          k_ref, v_ref, p_ref, qhm_ref, kaug_ref, vhm_ref, kmean_ref, pool_ref,
                          carry_ref):
    i = pl.program_id(0)
    tm = x_ref.shape[0]
    q, k, v, p = _qkvp(x_ref, g1_ref, win_ref, qg_ref, kg_ref, ones_ref, c_ref, sa_ref, sb_ref)
    k_ref[...] = k
    v_ref[...] = v
    p_ref[...] = p

    row = i * tm + lax.broadcasted_iota(jnp.int32, (tm, HEAD_DIM), 0)
    lane = lax.broadcasted_iota(jnp.int32, (tm, HEAD_DIM), 1)
    onehot = jnp.where(lane == row // MOBA_BLOCK, 1.0, 0.0).astype(BF16)
    qb, kb, vb = q.astype(BF16), k.astype(BF16), v.astype(BF16)
    for h in range(N_HEADS):
        sl = slice(h * HEAD_DIM, (h + 1) * HEAD_DIM)
        qhm_ref[h] = qb[:, sl]
        kaug_ref[h] = jnp.concatenate([kb[:, sl], onehot], axis=1)
        vhm_ref[h] = vb[:, sl]
    for b in range(tm // MOBA_BLOCK):
        kmean_ref[0, b:b + 1, :] = jnp.mean(k[b * MOBA_BLOCK:(b + 1) * MOBA_BLOCK], axis=0, keepdims=True)

    @pl.when(i == 0)
    def _():
        carry_ref[...] = jnp.zeros_like(carry_ref)

    ext = jnp.concatenate([carry_ref[...], p], axis=0)
    carry_ref[...] = p[tm - POOL_MAX:, :]
    sums = {}
    cur, w = ext, 1
    while w < POOL_MAX:
        cur = cur + pltpu.roll(cur, w, 0)
        w *= 2
        sums[w] = cur
    pos1 = (i * tm + lax.broadcasted_iota(jnp.int32, (tm, POOL_GROUP_WIDTH), 0) + 1).astype(F32)
    zs = []
    for g, w in enumerate(POOL_WINDOWS):
        sl = slice(g * POOL_GROUP_WIDTH, (g + 1) * POOL_GROUP_WIDTH)
        cnt = jnp.minimum(float(w), pos1)
        zs.append(sums[w][POOL_MAX:, sl] / cnt - p[:, sl])
    pool_ref[...] = _pool_project(zs, wpool_ref, pscale_ref).astype(BF16)


def _inproj_sample_kernel(x_ref, g1_ref, win_ref, qg_ref, kg_ref, ones_ref, c_ref, sa_ref, sb_ref,
                          q_ref, k_ref, v_ref, p_ref):
    q, k, v, p = _qkvp(x_ref, g1_ref, win_ref, qg_ref, kg_ref, ones_ref, c_ref, sa_ref, sb_ref)
    q_ref[...] = q
    k_ref[...] = k
    v_ref[...] = v
    p_ref[...] = p


def _select_topk_bias(gate, n_valid):
    n = gate.shape[1]
    col = lax.broadcasted_iota(jnp.int32, gate.shape, 1)
    neg_inf = jnp.float32(-jnp.inf)
    gate = jnp.where(col < n_valid, gate, neg_inf)
    bias = jnp.full(gate.shape, MASK_NEG, F32)
    for _ in range(MOBA_TOPK):
        mx = jnp.max(gate, axis=1, keepdims=True)
        idx = jnp.min(jnp.where(gate == mx, col, n), axis=1, keepdims=True)
        hit = (col == idx) & (mx > neg_inf)
        bias = jnp.where(hit, 0.0, bias)
        gate = jnp.where(hit, neg_inf, gate)
    return bias.astype(BF16)


def _attn_prompt_kernel(q_ref, kaug_ref, v_ref, kmt_ref, o_ref):
    i = pl.program_id(1)
    tq = q_ref.shape[1]
    outs = []
    for hh in range(q_ref.shape[0]):
        q = q_ref[hh]
        bias = _select_topk_bias(_dot(q, kmt_ref[hh]), i)
        qa = jnp.concatenate([q, bias], axis=1)

        start = pl.multiple_of(i * tq, tq)
        k_own = kaug_ref[hh, pl.ds(start, tq), :]
        v_own = v_ref[hh, pl.ds(start, tq), :]
        s = _dot_nt(q, k_own[:, :HEAD_DIM])
        r_id = lax.broadcasted_iota(jnp.int32, s.shape, 0)
        c_id = lax.broadcasted_iota(jnp.int32, s.shape, 1)
        s = jnp.where(c_id <= r_id, s, -jnp.inf)
        m = jnp.max(s, axis=1, keepdims=True)
        p = jnp.exp(s - m)
        l = jnp.sum(p, axis=1, keepdims=True)
        acc = _dot(p.astype(BF16), v_own)

        def body(j, carry, hh=hh, qa=qa):
            m, l, acc = carry
            off = pl.multiple_of(j * tq, tq)
            kj = kaug_ref[hh, pl.ds(off, tq), :]
            vj = v_ref[hh, pl.ds(off, tq), :]
            s = _dot_nt(qa, kj)
            m_new = jnp.maximum(m, jnp.max(s, axis=1, keepdims=True))
            alpha = jnp.exp(m - m_new)
            p = jnp.exp(s - m_new)
            l = alpha * l + jnp.sum(p, axis=1, keepdims=True)
            acc = alpha * acc + _dot(p.astype(BF16), vj)
            return m_new, l, acc

        m, l, acc = lax.fori_loop(0, i, body, (m, l, acc))
        outs.append(acc / l)
    o_ref[...] = jnp.concatenate(outs, axis=1).astype(o_ref.dtype)


def _kmean_sample_kernel(pt_ref, *refs):
    n_pg = SAMPLE_CHUNK_PAGES
    pages, o_ref = refs[:n_pg], refs[n_pg]
    ppb = MOBA_BLOCK // pages[0].shape[0]
    for b in range(n_pg // ppb):
        s = sum(jnp.sum(pages[b * ppb + u][...], axis=0, keepdims=True) for u in range(ppb))
        o_ref[b:b + 1, :] = s * (1.0 / MOBA_BLOCK)


def _attn_sample_kernel(pt_ref, qbd_ref, km_ref, kn_ref, vn_ref, *refs, n_blocks):
    n_pg = SAMPLE_CHUNK_PAGES
    kp, vp = refs[:n_pg], refs[n_pg:2 * n_pg]
    o_ref = refs[2 * n_pg]
    m_ref, l_ref, acc_ref, bias_ref = refs[2 * n_pg + 1:]
    c = pl.program_id(1)
    q = qbd_ref[...]
    n_rows = q.shape[0]
    ds_len = kn_ref.shape[0]
    page = kp[0].shape[0]
    blocks_per_chunk = n_pg * page // MOBA_BLOCK

    @pl.when(c == 0)
    def _():
        bias_ref[...] = _select_topk_bias(_dot_nt(q, km_ref[...]), n_blocks)
        s = _dot_nt(q, kn_ref[...].astype(BF16))
        t_q = lax.broadcasted_iota(jnp.int32, s.shape, 0) % ds_len
        t_k = lax.broadcasted_iota(jnp.int32, s.shape, 1)
        s = jnp.where(t_k <= t_q, s, -jnp.inf)
        m = jnp.max(s, axis=1, keepdims=True)
        p = jnp.exp(s - m)
        m_ref[...] = jnp.broadcast_to(m, m_ref.shape)
        l_ref[...] = jnp.broadcast_to(jnp.sum(p, axis=1, keepdims=True), l_ref.shape)
        acc_ref[...] = _dot(p.astype(BF16), vn_ref[...].astype(BF16))

    kc = jnp.concatenate([r[...] for r in kp], axis=0).astype(BF16)
    vc = jnp.concatenate([r[...] for r in vp], axis=0).astype(BF16)
    rows = kc.shape[0]
    blk = c * blocks_per_chunk + lax.broadcasted_iota(jnp.int32, (rows, LANES), 0) // MOBA_BLOCK
    onehot = jnp.where(lax.broadcasted_iota(jnp.int32, (rows, LANES), 1) == blk, 1.0, 0.0).astype(BF16)
    s = _dot_nt(q, kc) + _dot_nt(bias_ref[...], onehot)
    m_old = m_ref[:, :1]
    m_new = jnp.maximum(m_old, jnp.max(s, axis=1, keepdims=True))
    alpha = jnp.exp(m_old - m_new)
    p = jnp.exp(s - m_new)
    l_new = alpha * l_ref[:, :1] + jnp.sum(p, axis=1, keepdims=True)
    acc = alpha * acc_ref[...] + _dot(p.astype(BF16), vc)
    m_ref[...] = jnp.broadcast_to(m_new, m_ref.shape)
    l_ref[...] = jnp.broadcast_to(l_new, l_ref.shape)
    acc_ref[...] = acc

    @pl.when(c == pl.num_programs(1) - 1)
    def _():
        o = acc / l_new
        parts = [o[h * ds_len:(h + 1) * ds_len, h * HEAD_DIM:(h + 1) * HEAD_DIM] for h in range(N_HEADS)]
        o_ref[...] = jnp.concatenate(parts, axis=1).astype(o_ref.dtype)
    del n_rows


def _pool_sample_kernel(buf_ref, wpool_ref, pscale_ref, o_ref, *, past):
    n_buf = buf_ref.shape[0]
    n_out = o_ref.shape[0]
    rows = [buf_ref[r] for r in range(n_buf)]
    sums = {}
    cur, w = rows, 1
    while w < POOL_MAX:
        cur = [cur[r] + cur[r - w] if r - w >= 0 else cur[r] for r in range(n_buf)]
        w *= 2
        sums[w] = cur
    for t in range(n_out):
        r = n_buf - n_out + t
        zs = []
        for g, w in enumerate(POOL_WINDOWS):
            sl = slice(g * POOL_GROUP_WIDTH, (g + 1) * POOL_GROUP_WIDTH)
            cnt = float(min(w, past + t + 1))
            zs.append(sums[w][r][:, sl] / cnt - rows[r][:, sl])
        o_ref[t] = _pool_project(zs, wpool_ref, pscale_ref).astype(o_ref.dtype)


def _cmp_exchange(v, i, l, descending):
    hi, lo = jnp.maximum(v[i], v[l]), jnp.minimum(v[i], v[l])
    v[i], v[l] = (hi, lo) if descending else (lo, hi)


def _bitonic_sort_desc(vals):
    v = list(vals)
    n = len(v)
    k = 2
    while k <= n:
        j = k // 2
        while j >= 1:
            for i in range(n):
                l = i ^ j
                if l > i:
                    _cmp_exchange(v, i, l, (i & k) == 0)
            j //= 2
        k *= 2
    return v


def _bitonic_merge_desc(vals):
    v = list(vals)
    n = len(v)
    j = n // 2
    while j >= 1:
        for i in range(n):
            l = i ^ j
            if l > i:
                _cmp_exchange(v, i, l, True)
        j //= 2
    return v


def _topk_sorted_desc(vals, k):
    groups = [_bitonic_sort_desc(vals[g:g + k]) for g in range(0, len(vals), k)]
    while len(groups) > 1:
        nxt = []
        for a, b in zip(groups[0::2], groups[1::2]):
            nxt.append(_bitonic_merge_desc([jnp.maximum(a[i], b[k - 1 - i]) for i in range(k)]))
        groups = nxt
    return groups[0]


_CAND_PAIRS = [(i, j) for i in range(PEER_TOPK) for j in range(PEER_TOPK) if (i + 1) * (j + 1) <= PEER_TOPK + 1]
_CAND_PAD = 64


def _mid_kernel(x_ref, attn_ref, pool_ref, woa_ref, wop_ref, g2_ref, wqt_ref, ski_ref, skh_ref,
                x2_ref, h2_ref, thr_ref, e1_ref, s2m_ref, e2_ref):
    x2 = x_ref[...] + _dot(attn_ref[...], woa_ref[...]) + _dot(pool_ref[...], wop_ref[...])
    x2_ref[...] = x2
    h2 = _rms(x2, g2_ref[...]).astype(BF16)
    h2_ref[...] = h2
    nk = PEER_HEADS * PEER_HALF
    qt = _dot_nt(wqt_ref[...], h2).astype(BF16)
    s1 = _dot(ski_ref[0], qt[:nk])
    s2 = _dot(ski_ref[1], qt[nk:])
    s2h = _dot(skh_ref[...], qt[nk:])
    hd = PEER_HEADS
    v1 = _topk_sorted_desc([s1[a * hd:(a + 1) * hd] for a in range(PEER_N_KEYS)], PEER_TOPK)
    v2 = _topk_sorted_desc([s2[a * hd:(a + 1) * hd] for a in range(PEER_N_KEYS)], PEER_TOPK)
    neg = jnp.full(v1[0].shape, -jnp.inf, F32)
    cands = [v1[i] + v2[j] for i, j in _CAND_PAIRS] + [neg] * (_CAND_PAD - len(_CAND_PAIRS))
    c = _bitonic_sort_desc(cands)
    tau = 0.5 * (c[PEER_TOPK - 1] + c[PEER_TOPK])
    m1, m2 = v1[0], v2[0]
    mm = m1 + m2
    z = jnp.exp(c[0] - mm)
    for r in range(1, PEER_TOPK):
        z = z + jnp.exp(c[r] - mm)
    rz = 1.0 / z
    last1, last2 = v1[PEER_TOPK - 1], v2[PEER_TOPK - 1]
    for a in range(PEER_N_KEYS):
        rows = s1[a * hd:(a + 1) * hd]
        thr_ref[a] = jnp.where(rows >= last1, tau - rows, jnp.inf)
        e1_ref[a] = jnp.exp(rows - m1) * rz
    for h in range(hd):
        blk = s2h[h * PEER_N_KEYS:(h + 1) * PEER_N_KEYS]
        s2m_ref[h] = jnp.where(blk >= last2[h:h + 1], blk, -jnp.inf)
        e2_ref[h] = jnp.exp(blk - m2[h:h + 1])


def _peer_kernel(x2_ref, h2_ref, thr_ref, e1_ref, s2m_ref, e2_ref, u_ref, vt_ref, y_ref, acc_ref):
    e = pl.program_id(1)

    @pl.when(e == 0)
    def _():
        acc_ref[...] = jnp.zeros_like(acc_ref)

    pre = _dot_nt(u_ref[...], h2_ref[...])
    act = 0.5 * pre * (1.0 + lax.erf(pre * math.sqrt(0.5)))
    a_per_tile = u_ref.shape[0] // PEER_N_KEYS
    parts = []
    for ai in range(a_per_tile):
        a = e * a_per_tile + ai
        g = None
        for h in range(PEER_HEADS):
            thr = thr_ref[a, h:h + 1, :]
            e1 = e1_ref[a, h:h + 1, :]
            term = jnp.where(s2m_ref[h] >= thr, e2_ref[h], 0.0) * e1
            g = term if g is None else g + term
        parts.append(g * act[ai * PEER_N_KEYS:(ai + 1) * PEER_N_KEYS])
    ga = jnp.concatenate(parts, axis=0).astype(BF16)
    acc_ref[...] += _dot(vt_ref[...], ga)

    @pl.when(e == pl.num_programs(1) - 1)
    def _():
        y_ref[...] = x2_ref[...] + acc_ref[...].T


def _const_spec(shape):
    nd = len(shape)
    return pl.BlockSpec(shape, lambda *_: (0,) * nd)


def _rope_tables(pos):
    half = ROPE_DIM // 2
    inv_freq = 1.0 / (ROPE_THETA ** (jnp.arange(half, dtype=F32) * (2.0 / ROPE_DIM)))
    ang = pos.astype(F32)[:, None] * inv_freq[None, :]
    cos, sin = jnp.cos(ang), jnp.sin(ang)
    n = pos.shape[0]
    ones = jnp.ones((n, HEAD_DIM - ROPE_DIM), F32)
    zeros = jnp.zeros((n, HEAD_DIM - ROPE_DIM), F32)
    zh = jnp.zeros((n, half), F32)
    c = jnp.concatenate([cos, cos, ones], axis=1)
    sa = jnp.concatenate([-sin, zh, zeros], axis=1)
    sb = jnp.concatenate([zh, sin, zeros], axis=1)
    reps = LANES // HEAD_DIM
    return tuple(jnp.tile(t, (1, reps)) for t in (c, sa, sb))


def _inproj_common_specs(tm, d_model, in_width):
    return [
        pl.BlockSpec((tm, d_model), lambda i: (i, 0)),
        _const_spec((1, d_model)),
        _const_spec((d_model, in_width)),
        _const_spec((1, ATT_WIDTH)),
        _const_spec((1, ATT_WIDTH)),
        _const_spec((ATT_WIDTH, ATT_WIDTH)),
        pl.BlockSpec((tm, LANES), lambda i: (i, 0)),
        pl.BlockSpec((tm, LANES), lambda i: (i, 0)),
        pl.BlockSpec((tm, LANES), lambda i: (i, 0)),
    ]


def _inproj_prompt(x, pos, w):
    n, d_model = x.shape
    tm = min(ROW_TILE, n)
    assert n % tm == 0 and tm % MOBA_BLOCK == 0
    in_width = w["win"].shape[1]
    c, sa, sb = _rope_tables(pos)
    bpt = tm // MOBA_BLOCK
    row_spec = pl.BlockSpec((tm, ATT_WIDTH), lambda i: (i, 0))
    out_shape = [
        jax.ShapeDtypeStruct((n, ATT_WIDTH), F32),
        jax.ShapeDtypeStruct((n, ATT_WIDTH), F32),
        jax.ShapeDtypeStruct((n, POOL_WIDTH), F32),
        jax.ShapeDtypeStruct((N_HEADS, n, HEAD_DIM), BF16),
        jax.ShapeDtypeStruct((N_HEADS, n, 2 * HEAD_DIM), BF16),
        jax.ShapeDtypeStruct((N_HEADS, n, HEAD_DIM), BF16),
        jax.ShapeDtypeStruct((n // tm, bpt, ATT_WIDTH), F32),
        jax.ShapeDtypeStruct((n, POOL_WIDTH), BF16),
    ]
    out_specs = [
        row_spec, row_spec, row_spec,
        pl.BlockSpec((N_HEADS, tm, HEAD_DIM), lambda i: (0, i, 0)),
        pl.BlockSpec((N_HEADS, tm, 2 * HEAD_DIM), lambda i: (0, i, 0)),
        pl.BlockSpec((N_HEADS, tm, HEAD_DIM), lambda i: (0, i, 0)),
        pl.BlockSpec((1, bpt, ATT_WIDTH), lambda i: (i, 0, 0)),
        row_spec,
    ]
    in_specs = _inproj_common_specs(tm, d_model, in_width) + [
        _const_spec(w["wpool"].shape), _const_spec((1, POOL_WIDTH))]
    return pl.pallas_call(
        _inproj_prompt_kernel,
        grid=(n // tm,),
        in_specs=in_specs,
        out_specs=out_specs,
        out_shape=out_shape,
        scratch_shapes=[pltpu.VMEM((POOL_MAX, POOL_WIDTH), F32)],
        compiler_params=pltpu.CompilerParams(dimension_semantics=("arbitrary",), vmem_limit_bytes=VMEM_LIMIT),
        name="inproj_prompt",
    )(x, w["g1"], w["win"], w["qg"], w["kg"], w["ones_bd"], c, sa, sb, w["wpool"], w["pscale"])


def _inproj_sample(x, pos, w):
    n, d_model = x.shape
    tm = min(ROW_TILE, n)
    assert n % tm == 0
    in_width = w["win"].shape[1]
    c, sa, sb = _rope_tables(pos)
    row_spec = pl.BlockSpec((tm, ATT_WIDTH), lambda i: (i, 0))
    return pl.pallas_call(
        _inproj_sample_kernel,
        grid=(n // tm,),
        in_specs=_inproj_common_specs(tm, d_model, in_width),
        out_specs=[row_spec] * 4,
        out_shape=[jax.ShapeDtypeStruct((n, ATT_WIDTH), F32)] * 4,
        compiler_params=pltpu.CompilerParams(dimension_semantics=("arbitrary",), vmem_limit_bytes=VMEM_LIMIT),
        name="inproj_sample",
    )(x, w["g1"], w["win"], w["qg"], w["kg"], w["ones_bd"], c, sa, sb)


def _attn_prompt(qhm, kaug, vhm, kmean):
    n_heads, s, _ = qhm.shape
    tq = MOBA_BLOCK
    nb = s // tq
    assert s % tq == 0 and nb <= HEAD_DIM
    hpb = 2
    kmt = kmean.reshape(nb, n_heads, HEAD_DIM).transpose(1, 2, 0)
    kmt = jnp.pad(kmt, ((0, 0), (0, 0), (0, HEAD_DIM - nb))).astype(BF16)
    return pl.pallas_call(
        _attn_prompt_kernel,
        grid=(n_heads // hpb, nb),
        in_specs=[
            pl.BlockSpec((hpb, tq, HEAD_DIM), lambda g, i: (g, i, 0)),
            pl.BlockSpec((hpb, s, 2 * HEAD_DIM), lambda g, i: (g, 0, 0)),
            pl.BlockSpec((hpb, s, HEAD_DIM), lambda g, i: (g, 0, 0)),
            pl.BlockSpec((hpb, HEAD_DIM, HEAD_DIM), lambda g, i: (g, 0, 0)),
        ],
        out_specs=pl.BlockSpec((tq, hpb * HEAD_DIM), lambda g, i: (i, g)),
        out_shape=jax.ShapeDtypeStruct((s, n_heads * HEAD_DIM), BF16),
        compiler_params=pltpu.CompilerParams(dimension_semantics=("arbitrary", "arbitrary"),
                                             vmem_limit_bytes=VMEM_LIMIT),
        name="attn_prompt",
    )(qhm, kaug, vhm, kmt)


def _page_specs(n_pg, page, width):
    def make(u):
        return pl.BlockSpec((None, page, width), lambda b, c, pt: (pt[b, c * n_pg + u], 0, 0))
    return [make(u) for u in range(n_pg)]


def _attn_sample(q, k_new, v_new, cache_k, cache_v, page_table):
    db, ds_len, _ = q.shape
    _, page, width = cache_k.shape
    n_pages = page_table.shape[1]
    n_pg = SAMPLE_CHUNK_PAGES
    past = n_pages * page
    assert n_pages % n_pg == 0 and MOBA_BLOCK % page == 0 and (n_pg * page) % MOBA_BLOCK == 0
    assert past % MOBA_BLOCK == 0 and ds_len <= MOBA_BLOCK
    n_chunks = n_pages // n_pg
    bpc = n_pg * page // MOBA_BLOCK
    n_blocks = past // MOBA_BLOCK
    assert MOBA_TOPK <= n_blocks <= LANES

    kmean = pl.pallas_call(
        _kmean_sample_kernel,
        grid_spec=pltpu.PrefetchScalarGridSpec(
            num_scalar_prefetch=1,
            grid=(db, n_chunks),
            in_specs=_page_specs(n_pg, page, width),
            out_specs=pl.BlockSpec((None, None, bpc, width), lambda b, c, pt: (b, c, 0, 0)),
        ),
        out_shape=jax.ShapeDtypeStruct((db, n_chunks, bpc, width), F32),
        compiler_params=pltpu.CompilerParams(dimension_semantics=("arbitrary", "arbitrary"),
                                             vmem_limit_bytes=VMEM_LIMIT),
        name="kmean_sample",
    )(page_table, *([cache_k] * n_pg))
    kmean = kmean.reshape(db, n_blocks, width)
    kmean = jnp.pad(kmean, ((0, 0), (0, LANES - n_blocks), (0, 0))).astype(BF16)

    q4 = q.reshape(db, ds_len, N_HEADS, HEAD_DIM)
    eye = jnp.eye(N_HEADS, dtype=F32)
    qbd = jnp.einsum("bthd,hg->bhtgd", q4, eye).reshape(db, N_HEADS * ds_len, width).astype(BF16)
    n_rows = N_HEADS * ds_len

    def seq_spec(r):
        return pl.BlockSpec((None, r, width), lambda b, c, pt: (b, 0, 0))

    return pl.pallas_call(
        functools.partial(_attn_sample_kernel, n_blocks=n_blocks),
        grid_spec=pltpu.PrefetchScalarGridSpec(
            num_scalar_prefetch=1,
            grid=(db, n_chunks),
            in_specs=[seq_spec(n_rows), seq_spec(LANES), seq_spec(ds_len), seq_spec(ds_len)]
            + _page_specs(n_pg, page, width) + _page_specs(n_pg, page, width),
            out_specs=seq_spec(ds_len),
            scratch_shapes=[
                pltpu.VMEM((n_rows, LANES), F32),
                pltpu.VMEM((n_rows, LANES), F32),
                pltpu.VMEM((n_rows, width), F32),
                pltpu.VMEM((n_rows, LANES), BF16),
            ],
        ),
        out_shape=jax.ShapeDtypeStruct((db, ds_len, width), BF16),
        compiler_params=pltpu.CompilerParams(dimension_semantics=("arbitrary", "arbitrary"),
                                             vmem_limit_bytes=VMEM_LIMIT),
        name="attn_sample",
    )(page_table, qbd, kmean, k_new, v_new, *([cache_k] * n_pg), *([cache_v] * n_pg))


def _pool_sample(state, p, w, past):
    db, ds_len, width = p.shape
    buf = jnp.concatenate([state, p], axis=1).transpose(1, 0, 2)
    n_buf = buf.shape[0]
    out = pl.pallas_call(
        functools.partial(_pool_sample_kernel, past=past),
        grid=(1,),
        in_specs=[_const_spec((n_buf, db, width)), _const_spec(w["wpool"].shape), _const_spec((1, POOL_WIDTH))],
        out_specs=_const_spec((ds_len, db, width)),
        out_shape=jax.ShapeDtypeStruct((ds_len, db, width), BF16),
        compiler_params=pltpu.CompilerParams(dimension_semantics=("arbitrary",), vmem_limit_bytes=VMEM_LIMIT),
        name="pool_sample",
    )(buf, w["wpool"], w["pscale"])
    return out.transpose(1, 0, 2)


def _mid(x, attn, pool, w):
    t, d_model = x.shape
    tt = MID_TILE
    assert t % tt == 0
    nk = PEER_HEADS * PEER_HALF
    tok = lambda width: pl.BlockSpec((tt, width), lambda i: (i, 0))
    ah = pl.BlockSpec((PEER_N_KEYS, PEER_HEADS, tt), lambda i: (0, 0, i))
    ha = pl.BlockSpec((PEER_HEADS, PEER_N_KEYS, tt), lambda i: (0, 0, i))
    return pl.pallas_call(
        _mid_kernel,
        grid=(t // tt,),
        in_specs=[
            tok(d_model), tok(ATT_WIDTH), tok(POOL_WIDTH),
            _const_spec((ATT_WIDTH, d_model)), _const_spec((POOL_WIDTH, d_model)),
            _const_spec((1, d_model)),
            _const_spec((2 * nk, d_model)),
            _const_spec((2, nk, nk)),
            _const_spec((nk, nk)),
        ],
        out_specs=[tok(d_model), tok(d_model), ah, ah, ha, ha],
        out_shape=[
            jax.ShapeDtypeStruct((t, d_model), F32),
            jax.ShapeDtypeStruct((t, d_model), BF16),
            jax.ShapeDtypeStruct((PEER_N_KEYS, PEER_HEADS, t), F32),
            jax.ShapeDtypeStruct((PEER_N_KEYS, PEER_HEADS, t), F32),
            jax.ShapeDtypeStruct((PEER_HEADS, PEER_N_KEYS, t), F32),
            jax.ShapeDtypeStruct((PEER_HEADS, PEER_N_KEYS, t), F32),
        ],
        compiler_params=pltpu.CompilerParams(dimension_semantics=("arbitrary",), vmem_limit_bytes=VMEM_LIMIT),
        name="outproj_route",
    )(x, attn, pool, w["woa"], w["wop"], w["g2"], w["wqt"], w["ski"], w["skh"])


def _peer(x2, h2, thr, e1, s2m, e2, w):
    t, d_model = x2.shape
    n_exp = w["u"].shape[0]
    tt = min(PEER_TOK_TILE, t)
    et = PEER_EXP_TILE
    assert t % tt == 0 and n_exp % et == 0 and et % PEER_N_KEYS == 0
    ah = pl.BlockSpec((PEER_N_KEYS, PEER_HEADS, tt), lambda i, e: (0, 0, i))
    ha = pl.BlockSpec((PEER_HEADS, PEER_N_KEYS, tt), lambda i, e: (0, 0, i))
    return pl.pallas_call(
        _peer_kernel,
        grid=(t // tt, n_exp // et),
        in_specs=[
            pl.BlockSpec((tt, d_model), lambda i, e: (i, 0)),
            pl.BlockSpec((tt, d_model), lambda i, e: (i, 0)),
            ah, ah, ha, ha,
            pl.BlockSpec((et, d_model), lambda i, e: (e, 0)),
            pl.BlockSpec((d_model, et), lambda i, e: (0, e)),
        ],
        out_specs=pl.BlockSpec((tt, d_model), lambda i, e: (i, 0)),
        out_shape=jax.ShapeDtypeStruct((t, d_model), F32),
        scratch_shapes=[pltpu.VMEM((d_model, tt), F32)],
        compiler_params=pltpu.CompilerParams(dimension_semantics=("arbitrary", "arbitrary"),
                                             vmem_limit_bytes=VMEM_LIMIT),
        name="peer_dense",
    )(x2, h2, thr, e1, s2m, e2, w["u"], w["vt"])


def _layer_weights(l, norm1_g, w_in, q_norm_g, k_norm_g, w_pool, pool_scale, w_out, norm2_g,
                   peer_wq, peer_subkeys, peer_u, peer_v):
    d_model = w_in.shape[1]
    eye = jnp.eye(PEER_HEADS, dtype=F32)
    sk = peer_subkeys[l]
    nk = PEER_HEADS * PEER_HALF
    wq = peer_wq[l].reshape(d_model, PEER_HEADS, 2, PEER_HALF).transpose(0, 2, 1, 3).reshape(d_model, 2 * nk)
    return {
        "g1": norm1_g[l][None],
        "win": w_in[l].astype(BF16),
        "qg": jnp.tile(q_norm_g[l], N_HEADS)[None],
        "kg": jnp.tile(k_norm_g[l], N_HEADS)[None],
        "ones_bd": jnp.asarray(np.kron(np.eye(N_HEADS), np.ones((HEAD_DIM, HEAD_DIM))), BF16),
        "wpool": w_pool[l].astype(BF16),
        "pscale": pool_scale[l][None],
        "woa": w_out[l][:ATT_WIDTH].astype(BF16),
        "wop": w_out[l][ATT_WIDTH:].astype(BF16),
        "g2": norm2_g[l][None],
        "wqt": wq.T.astype(BF16),
        "ski": jnp.einsum("hckd,hg->ckhgd", sk, eye).reshape(2, nk, nk).astype(BF16),
        "skh": jnp.einsum("hkd,hg->hkgd", sk[:, 1], eye).reshape(nk, nk).astype(BF16),
        "u": peer_u[l].astype(BF16),
        "vt": peer_v[l].T.astype(BF16),
    }


def kernel(x_prompt, x_sample, cache_k, cache_v, state_pool_rows, page_table, norm1_g, w_in, q_norm_g, k_norm_g,
           w_pool, pool_scale, w_out, norm2_g, peer_wq, peer_subkeys, peer_u, peer_v):
    b, s, d_model = x_prompt.shape
    db, ds_len, _ = x_sample.shape
    depth, n_pool, page = cache_k.shape[:3]
    assert b == 1
    past = page_table.shape[1] * page
    pos_p = jnp.arange(s)
    pos_s = jnp.tile(past + jnp.arange(ds_len), db)
    yp = x_prompt.reshape(s, d_model)
    ys = x_sample.reshape(db * ds_len, d_model)
    outs = [[] for _ in range(6)]
    for l in range(depth):
        w = _layer_weights(l, norm1_g, w_in, q_norm_g, k_norm_g, w_pool, pool_scale, w_out, norm2_g,
                           peer_wq, peer_subkeys, peer_u, peer_v)
        kp, vp, pp, qhm, kaug, vhm, kmean, pool_p = _inproj_prompt(yp, pos_p, w)
        attn_p = _attn_prompt(qhm, kaug, vhm, kmean.reshape(-1, ATT_WIDTH))
        qs, ks, vs, ps = _inproj_sample(ys, pos_s, w)
        ck = cache_k[l].reshape(n_pool, page, ATT_WIDTH)
        cv = cache_v[l].reshape(n_pool, page, ATT_WIDTH)
        r3 = lambda a: a.reshape(db, ds_len, -1)
        attn_s = _attn_sample(r3(qs), r3(ks), r3(vs), ck, cv, page_table)
        pool_s = _pool_sample(state_pool_rows[l], r3(ps), w, past)
        x_all = jnp.concatenate([yp, ys], axis=0)
        attn_all = jnp.concatenate([attn_p, attn_s.reshape(db * ds_len, -1)], axis=0)
        pool_all = jnp.concatenate([pool_p, pool_s.reshape(db * ds_len, -1)], axis=0)
        x2, h2, thr, e1, s2m, e2 = _mid(x_all, attn_all, pool_all, w)
        y_all = _peer(x2, h2, thr, e1, s2m, e2, w)
        yp, ys = y_all[:s], y_all[s:]
        outs[0].append(kp.reshape(b, s, N_HEADS, HEAD_DIM))
        outs[1].append(vp.reshape(b, s, N_HEADS, HEAD_DIM))
        outs[2].append(pp[s - POOL_BUF:].reshape(b, POOL_BUF, POOL_WIDTH))
        outs[3].append(ks.reshape(db, ds_len, N_HEADS, HEAD_DIM))
        outs[4].append(vs.reshape(db, ds_len, N_HEADS, HEAD_DIM))
        buf = jnp.concatenate([state_pool_rows[l], r3(ps)], axis=1)
        outs[5].append(buf[:, buf.shape[1] - POOL_BUF:])
    return (yp.reshape(b, s, d_model), ys.reshape(db, ds_len, d_model),
            *[jnp.stack(o) for o in outs])
```

```python
import functools
import math

import jax
import jax.numpy as jnp
import numpy as np
from jax import lax
from jax.experimental import pallas as pl
from jax.experimental.pallas import tpu as pltpu

F32 = jnp.float32
BF16 = jnp.bfloat16

N_HEADS = 8
HEAD_DIM = 64
ATT_WIDTH = N_HEADS * HEAD_DIM
MOBA_BLOCK = 256
MOBA_TOPK = 3
ROPE_DIM = HEAD_DIM // 4
ROPE_THETA = 500000.0
POOL_WINDOWS = (2, 4, 8, 16)
POOL_GROUP_WIDTH = 128
POOL_WIDTH = POOL_GROUP_WIDTH * len(POOL_WINDOWS)
POOL_MAX = max(POOL_WINDOWS)
POOL_BUF = POOL_MAX - 1
PEER_HEADS = 8
PEER_HALF = 128
PEER_N_KEYS = 128
PEER_TOPK = 16
NORM_EPS = 1e-6

LANES = 128
SUBLANES = 8
MASK_NEG = -1e30
VMEM_LIMIT = 56 * 1024 * 1024

ROW_TILE = 512
MID_TILE = 256
PEER_TOK_TILE = 512
PEER_EXP_TILE = 512
SAMPLE_CHUNK_PAGES = 16
ATTN_BLOCKS_PER_ITER = 2
PEER_ROW_CHUNK = 16


def _dot(a, b):
    return jnp.dot(a, b, preferred_element_type=F32)


def _dot_nt(a, b):
    return lax.dot_general(a, b, (((1,), (1,)), ((), ())), preferred_element_type=F32)


def _rms(x, g):
    ms = jnp.mean(x * x, axis=-1, keepdims=True)
    return x * lax.rsqrt(ms + NORM_EPS) * g


def _head_rms(x, ones_bd, g_tiled):
    sq = x * x
    hi = sq.astype(BF16)
    lo = (sq - hi.astype(F32)).astype(BF16)
    ss = _dot(hi, ones_bd) + _dot(lo, ones_bd)
    return x * lax.rsqrt(ss * (1.0 / HEAD_DIM) + NORM_EPS) * g_tiled


def _rope(x, c, sa, sb):
    reps = x.shape[1] // LANES
    c4 = jnp.concatenate([c] * reps, axis=1)
    sa4 = jnp.concatenate([sa] * reps, axis=1)
    sb4 = jnp.concatenate([sb] * reps, axis=1)
    half = ROPE_DIM // 2
    return x * c4 + pltpu.roll(x, x.shape[1] - half, 1) * sa4 + pltpu.roll(x, half, 1) * sb4


def _qkvp(x_ref, g1_ref, win_ref, qg_ref, kg_ref, ones_ref, c_ref, sa_ref, sb_ref):
    h = _rms(x_ref[...], g1_ref[...]).astype(BF16)
    z = _dot(h, win_ref[...])
    a = ATT_WIDTH
    q, k, v, p = z[:, :a], z[:, a:2 * a], z[:, 2 * a:3 * a], z[:, 3 * a:]
    ones_bd = ones_ref[...]
    c, sa, sb = c_ref[...], sa_ref[...], sb_ref[...]
    q = _rope(_head_rms(q, ones_bd, qg_ref[...]), c, sa, sb) * (HEAD_DIM ** -0.5)
    k = _rope(_head_rms(k, ones_bd, kg_ref[...]), c, sa, sb)
    return q, k, v, p


def _pool_project(zs, wpool_ref, pscale_ref):
    outs = [_dot(z.astype(BF16), wpool_ref[g]) for g, z in enumerate(zs)]
    return jnp.concatenate(outs, axis=1) * pscale_ref[...]


def _inproj_prompt_kernel(x_ref, g1_ref, win_ref, qg_ref, kg_ref, ones_ref, c_ref, sa_ref, sb_ref,
                          wpool_ref, pscale_ref,
                          k_ref, v_ref, p_ref, qhm_ref, kaug_ref, vhm_ref, kmean_ref, pool_ref,
                          carry_ref):
    i = pl.program_id(0)
    tm = x_ref.shape[0]
    q, k, v, p = _qkvp(x_ref, g1_ref, win_ref, qg_ref, kg_ref, ones_ref, c_ref, sa_ref, sb_ref)
    k_ref[...] = k
    v_ref[...] = v
    p_ref[...] = p

    row = i * tm + lax.broadcasted_iota(jnp.int32, (tm, HEAD_DIM), 0)
    lane = lax.broadcasted_iota(jnp.int32, (tm, HEAD_DIM), 1)
    onehot = jnp.where(lane == row // MOBA_BLOCK, 1.0, 0.0).astype(BF16)
    ones = jnp.ones((tm, HEAD_DIM), BF16)
    qb, kb, vb = q.astype(BF16), k.astype(BF16), v.astype(BF16)
    for h in range(N_HEADS):
        sl = slice(h * HEAD_DIM, (h + 1) * HEAD_DIM)
        qhm_ref[h] = qb[:, sl]
        kaug_ref[h] = jnp.concatenate([kb[:, sl], onehot], axis=1)
        vhm_ref[h] = jnp.concatenate([vb[:, sl], ones], axis=1)
    for b in range(tm // MOBA_BLOCK):
        kmean_ref[0, b:b + 1, :] = jnp.mean(k[b * MOBA_BLOCK:(b + 1) * MOBA_BLOCK], axis=0, keepdims=True)

    @pl.when(i == 0)
    def _():
        carry_ref[...] = jnp.zeros_like(carry_ref)

    ext = jnp.concatenate([carry_ref[...], p], axis=0)
    carry_ref[...] = p[tm - POOL_MAX:, :]
    sums = {}
    cur, w = ext, 1
    while w < POOL_MAX:
        cur = cur + pltpu.roll(cur, w, 0)
        w *= 2
        sums[w] = cur
    pos1 = (i * tm + lax.broadcasted_iota(jnp.int32, (tm, POOL_GROUP_WIDTH), 0) + 1).astype(F32)
    zs = []
    for g, w in enumerate(POOL_WINDOWS):
        sl = slice(g * POOL_GROUP_WIDTH, (g + 1) * POOL_GROUP_WIDTH)
        cnt = jnp.minimum(float(w), pos1)
        zs.append(sums[w][POOL_MAX:, sl] / cnt - p[:, sl])
    pool_ref[...] = _pool_project(zs, wpool_ref, pscale_ref).astype(BF16)


def _inproj_sample_kernel(x_ref, g1_ref, win_ref, qg_ref, kg_ref, ones_ref, c_ref, sa_ref, sb_ref,
                          q_ref, k_ref, v_ref, p_ref):
    q, k, v, p = _qkvp(x_ref, g1_ref, win_ref, qg_ref, kg_ref, ones_ref, c_ref, sa_ref, sb_ref)
    q_ref[...] = q
    k_ref[...] = k
    v_ref[...] = v
    p_ref[...] = p


def _select_topk_bias(gate, n_valid):
    n = gate.shape[1]
    col = lax.broadcasted_iota(jnp.int32, gate.shape, 1)
    neg_inf = jnp.float32(-jnp.inf)
    gate = jnp.where(col < n_valid, gate, neg_inf)
    bias = jnp.full(gate.shape, MASK_NEG, F32)
    for _ in range(MOBA_TOPK):
        mx = jnp.max(gate, axis=1, keepdims=True)
        idx = jnp.min(jnp.where(gate == mx, col, n), axis=1, keepdims=True)
        hit = (col == idx) & (mx > neg_inf)
        bias = jnp.where(hit, 0.0, bias)
        gate = jnp.where(hit, neg_inf, gate)
    return bias


def _attn_prompt_kernel(q_ref, kaug_ref, vaug_ref, kmt_ref, o_ref):
    i = pl.program_id(1)
    tq = q_ref.shape[1]
    n_h = q_ref.shape[0]
    kv_chunk = ATTN_BLOCKS_PER_ITER * tq
    qas, carry = [], []
    start = pl.multiple_of(i * tq, tq)
    for hh in range(n_h):
        q = q_ref[hh]
        bias = _select_topk_bias(_dot(q, kmt_ref[hh]), i).astype(BF16)
        qas.append(jnp.concatenate([q, bias], axis=1))
        k_own = kaug_ref[hh, pl.ds(start, tq), :]
        s = _dot_nt(q, k_own[:, :HEAD_DIM])
        r_id = lax.broadcasted_iota(jnp.int32, s.shape, 0)
        c_id = lax.broadcasted_iota(jnp.int32, s.shape, 1)
        s = jnp.where(c_id <= r_id, s, -jnp.inf)
        m = jnp.max(s, axis=1, keepdims=True)
        p = jnp.exp(s - m)
        carry += [m, _dot(p.astype(BF16), vaug_ref[hh, pl.ds(start, tq), :])]

    def scores(hh, it):
        off = pl.multiple_of(it * kv_chunk, kv_chunk)
        return _dot_nt(qas[hh], kaug_ref[hh, pl.ds(off, kv_chunk), :])

    last_chunk = kaug_ref.shape[1] // kv_chunk - 1

    def body(it, carry):
        off = pl.multiple_of(it * kv_chunk, kv_chunk)
        nxt = jnp.minimum(it + 1, last_chunk)
        new = []
        for hh in range(n_h):
            m, acc, s = carry[3 * hh:3 * hh + 3]
            s_next = scores(hh, nxt)
            m_new = jnp.maximum(m, jnp.max(s, axis=1, keepdims=True))
            p = jnp.exp(s - m_new)
            acc = jnp.exp(m - m_new) * acc + _dot(p.astype(BF16), vaug_ref[hh, pl.ds(off, kv_chunk), :])
            new += [m_new, acc, s_next]
        return tuple(new)

    n_iter = (i + ATTN_BLOCKS_PER_ITER - 1) // ATTN_BLOCKS_PER_ITER
    init = []
    for hh in range(n_h):
        init += [carry[2 * hh], carry[2 * hh + 1], scores(hh, 0)]
    carry = lax.fori_loop(0, n_iter, body, tuple(init))
    outs = [carry[3 * hh + 1][:, :HEAD_DIM] / carry[3 * hh + 1][:, HEAD_DIM:HEAD_DIM + 1] for hh in range(n_h)]
    o_ref[...] = jnp.concatenate(outs, axis=1).astype(o_ref.dtype)


def _attn_sample_kernel(pt_ref, qbd_ref, kn_ref, vn_ref, *refs, n_blocks, n_pg):
    kp, vp = refs[:n_pg], refs[n_pg:2 * n_pg]
    o_ref = refs[2 * n_pg]
    s_ref, p_ref, acc_ref, l_ref = refs[2 * n_pg + 1:]
    c = pl.program_id(1)
    page = kp[0].shape[-1]
    chunk = n_pg * page
    n_kc = s_ref.shape[1] // chunk
    ds_len = kn_ref.shape[0]
    n_rows = qbd_ref.shape[0]

    @pl.when(c < n_kc)
    def _():
        kt = jnp.concatenate([r[...].reshape(ATT_WIDTH, page) for r in kp], axis=1).astype(BF16)
        s_ref[:, pl.ds(pl.multiple_of(c * chunk, chunk), chunk)] = _dot(qbd_ref[...], kt)

    @pl.when(c == n_kc - 1)
    def _():
        col = lax.broadcasted_iota(jnp.int32, (n_rows, LANES), 1)
        gate = jnp.zeros((n_rows, LANES), F32)
        for n in range(n_blocks):
            g = jnp.sum(s_ref[:, n * MOBA_BLOCK:(n + 1) * MOBA_BLOCK], axis=1, keepdims=True)
            gate = jnp.where(col == n, g * (1.0 / MOBA_BLOCK), gate)
        bias = _select_topk_bias(gate, n_blocks)
        s_own = _dot_nt(qbd_ref[...], kn_ref[...].astype(BF16))
        t_q = lax.broadcasted_iota(jnp.int32, s_own.shape, 0) % ds_len
        t_k = lax.broadcasted_iota(jnp.int32, s_own.shape, 1)
        s_own = jnp.where(t_k <= t_q, s_own, -jnp.inf)
        m = jnp.max(s_own, axis=1, keepdims=True)
        for n in range(n_blocks):
            sl = slice(n * MOBA_BLOCK, (n + 1) * MOBA_BLOCK)
            sn = s_ref[:, sl] + bias[:, n:n + 1]
            s_ref[:, sl] = sn
            m = jnp.maximum(m, jnp.max(sn, axis=1, keepdims=True))
        p_own = jnp.exp(s_own - m)
        l = jnp.sum(p_own, axis=1, keepdims=True)
        for n in range(n_blocks):
            sl = slice(n * MOBA_BLOCK, (n + 1) * MOBA_BLOCK)
            pn = jnp.exp(s_ref[:, sl] - m)
            l = l + jnp.sum(pn, axis=1, keepdims=True)
            p_ref[:, sl] = pn.astype(BF16)
        l_ref[...] = jnp.broadcast_to(l, l_ref.shape)
        acc_ref[...] = _dot(p_own.astype(BF16), vn_ref[...].astype(BF16))

    @pl.when(c >= n_kc)
    def _():
        vt = jnp.concatenate([r[...].reshape(ATT_WIDTH, page) for r in vp], axis=1).astype(BF16)
        off = pl.multiple_of((c - n_kc) * chunk, chunk)
        acc_ref[...] += _dot_nt(p_ref[:, pl.ds(off, chunk)], vt)

    @pl.when(c == 2 * n_kc - 1)
    def _():
        o = acc_ref[...] / l_ref[:, :1]
        parts = [o[h * ds_len:(h + 1) * ds_len, h * HEAD_DIM:(h + 1) * HEAD_DIM] for h in range(N_HEADS)]
        o_ref[...] = jnp.concatenate(parts, axis=1).astype(o_ref.dtype)


def _pool_sample_kernel(buf_ref, wpool_ref, pscale_ref, o_ref, *, past):
    n_buf = buf_ref.shape[0]
    n_out = o_ref.shape[0]
    rows = [buf_ref[r] for r in range(n_buf)]
    sums = {}
    cur, w = rows, 1
    while w < POOL_MAX:
        cur = [cur[r] + cur[r - w] if r - w >= 0 else cur[r] for r in range(n_buf)]
        w *= 2
        sums[w] = cur
    for t in range(n_out):
        r = n_buf - n_out + t
        zs = []
        for g, w in enumerate(POOL_WINDOWS):
            sl = slice(g * POOL_GROUP_WIDTH, (g + 1) * POOL_GROUP_WIDTH)
            cnt = float(min(w, past + t + 1))
            zs.append(sums[w][r][:, sl] / cnt - rows[r][:, sl])
        o_ref[t] = _pool_project(zs, wpool_ref, pscale_ref).astype(o_ref.dtype)


def _cmp_exchange(v, i, l, descending):
    hi, lo = jnp.maximum(v[i], v[l]), jnp.minimum(v[i], v[l])
    v[i], v[l] = (hi, lo) if descending else (lo, hi)


def _bitonic_sort_desc(vals):
    v = list(vals)
    n = len(v)
    k = 2
    while k <= n:
        j = k // 2
        while j >= 1:
            for i in range(n):
                l = i ^ j
                if l > i:
                    _cmp_exchange(v, i, l, (i & k) == 0)
            j //= 2
        k *= 2
    return v


def _bitonic_merge_desc(vals):
    v = list(vals)
    n = len(v)
    j = n // 2
    while j >= 1:
        for i in range(n):
            l = i ^ j
            if l > i:
                _cmp_exchange(v, i, l, True)
        j //= 2
    return v


def _topk_sorted_desc(vals, k):
    groups = [_bitonic_sort_desc(vals[g:g + k]) for g in range(0, len(vals), k)]
    while len(groups) > 1:
        nxt = []
        for a, b in zip(groups[0::2], groups[1::2]):
            nxt.append(_bitonic_merge_desc([jnp.maximum(a[i], b[k - 1 - i]) for i in range(k)]))
        groups = nxt
    return groups[0]


_CAND_PAIRS = [(i, j) for i in range(PEER_TOPK) for j in range(PEER_TOPK) if (i + 1) * (j + 1) <= PEER_TOPK + 1]
_CAND_PAD = 64


def _mid_kernel(x_ref, attn_ref, pool_ref, woa_ref, wop_ref, g2_ref, wqt_ref, ski_ref, skh_ref,
                x2_ref, h2_ref, ethr_ref, e1_ref, e2m_ref):
    x2 = x_ref[...] + _dot(attn_ref[...], woa_ref[...]) + _dot(pool_ref[...], wop_ref[...])
    x2_ref[...] = x2
    h2 = _rms(x2, g2_ref[...]).astype(BF16)
    h2_ref[...] = h2
    nk = PEER_HEADS * PEER_HALF
    qt = _dot_nt(wqt_ref[...], h2).astype(BF16)
    s1 = _dot(ski_ref[0], qt[:nk])
    s2 = _dot(ski_ref[1], qt[nk:])
    s2h = _dot(skh_ref[...], qt[nk:])
    hd = PEER_HEADS
    v1 = _topk_sorted_desc([s1[a * hd:(a + 1) * hd] for a in range(PEER_N_KEYS)], PEER_TOPK)
    v2 = _topk_sorted_desc([s2[a * hd:(a + 1) * hd] for a in range(PEER_N_KEYS)], PEER_TOPK)
    neg = jnp.full(v1[0].shape, -jnp.inf, F32)
    cands = [v1[i] + v2[j] for i, j in _CAND_PAIRS] + [neg] * (_CAND_PAD - len(_CAND_PAIRS))
    c = _bitonic_sort_desc(cands)
    tau = 0.5 * (c[PEER_TOPK - 1] + c[PEER_TOPK])
    m1, m2 = v1[0], v2[0]
    mm = m1 + m2
    z = jnp.exp(c[0] - mm)
    for r in range(1, PEER_TOPK):
        z = z + jnp.exp(c[r] - mm)
    rz = 1.0 / z
    last1, last2 = v1[PEER_TOPK - 1], v2[PEER_TOPK - 1]
    for a in range(PEER_N_KEYS):
        rows = s1[a * hd:(a + 1) * hd]
        ethr_ref[a] = jnp.exp(jnp.where(rows >= last1, tau - rows, jnp.inf) - m2)
        e1_ref[a] = jnp.exp(rows - m1) * rz
    for h in range(hd):
        blk = s2h[h * PEER_N_KEYS:(h + 1) * PEER_N_KEYS]
        e2m_ref[h] = jnp.where(blk >= last2[h:h + 1], jnp.exp(blk - m2[h:h + 1]), 0.0)


def _peer_kernel(x2_ref, h2_ref, ethr_ref, e1_ref, e2m_ref, u_ref, vt_ref, y_ref, acc_ref, act_ref, ga_ref):
    s = pl.program_id(1)
    n_tiles = pl.num_programs(1) - 2
    cur, prev = s % 2, (s + 1) % 2
    tt = h2_ref.shape[0]
    half = tt // 2
    a_per_tile = u_ref.shape[0] // PEER_N_KEYS
    gate_tile = jnp.clip(s - 1, 0, n_tiles - 1)

    @pl.when(s == 0)
    def _():
        acc_ref[...] = jnp.zeros_like(acc_ref)
        act_ref[1] = jnp.zeros(act_ref.shape[1:], act_ref.dtype)
        ga_ref[0] = jnp.zeros(ga_ref.shape[1:], ga_ref.dtype)

    def second_matmul(t0):
        acc_ref[:, t0:t0 + half] += _dot(vt_ref[...], ga_ref[cur, :, t0:t0 + half])

    def first_matmul(t0):
        pre = _dot_nt(u_ref[...], h2_ref[t0:t0 + half, :])
        act_ref[cur, :, t0:t0 + half] = 0.5 * pre * (1.0 + lax.erf(pre * math.sqrt(0.5)))

    def gates(ai):
        a = gate_tile * a_per_tile + ai
        rows = slice(ai * PEER_N_KEYS, (ai + 1) * PEER_N_KEYS)
        for t0 in range(0, tt, LANES):
            ts = slice(t0, t0 + LANES)
            g = None
            for h in range(PEER_HEADS):
                x = e2m_ref[h, :, ts]
                term = jnp.where(x >= ethr_ref[a, h:h + 1, ts], x, 0.0) * e1_ref[a, h:h + 1, ts]
                g = term if g is None else g + term
            ga_ref[prev, rows, ts] = (g * act_ref[prev, rows, ts]).astype(BF16)

    mxu_pieces = [lambda: second_matmul(0), lambda: second_matmul(half),
                  lambda: first_matmul(0), lambda: first_matmul(half)]
    for ai in range(a_per_tile):
        if ai < len(mxu_pieces):
            mxu_pieces[ai]()
        gates(ai)
    for piece in mxu_pieces[a_per_tile:]:
        piece()

    @pl.when(s == n_tiles + 1)
    def _():
        y_ref[...] = x2_ref[...] + acc_ref[...].T


def _const_spec(shape):
    nd = len(shape)
    return pl.BlockSpec(shape, lambda *_: (0,) * nd)


def _rope_tables(pos):
    half = ROPE_DIM // 2
    inv_freq = 1.0 / (ROPE_THETA ** (jnp.arange(half, dtype=F32) * (2.0 / ROPE_DIM)))
    ang = pos.astype(F32)[:, None] * inv_freq[None, :]
    cos, sin = jnp.cos(ang), jnp.sin(ang)
    n = pos.shape[0]
    ones = jnp.ones((n, HEAD_DIM - ROPE_DIM), F32)
    zeros = jnp.zeros((n, HEAD_DIM - ROPE_DIM), F32)
    zh = jnp.zeros((n, half), F32)
    c = jnp.concatenate([cos, cos, ones], axis=1)
    sa = jnp.concatenate([-sin, zh, zeros], axis=1)
    sb = jnp.concatenate([zh, sin, zeros], axis=1)
    reps = LANES // HEAD_DIM
    return tuple(jnp.tile(t, (1, reps)) for t in (c, sa, sb))


def _inproj_common_specs(tm, d_model, in_width):
    return [
        pl.BlockSpec((tm, d_model), lambda i: (i, 0)),
        _const_spec((1, d_model)),
        _const_spec((d_model, in_width)),
        _const_spec((1, ATT_WIDTH)),
        _const_spec((1, ATT_WIDTH)),
        _const_spec((ATT_WIDTH, ATT_WIDTH)),
        pl.BlockSpec((tm, LANES), lambda i: (i, 0)),
        pl.BlockSpec((tm, LANES), lambda i: (i, 0)),
        pl.BlockSpec((tm, LANES), lambda i: (i, 0)),
    ]


def _inproj_prompt(x, pos, w):
    n, d_model = x.shape
    tm = min(ROW_TILE, n)
    assert n % tm == 0 and tm % MOBA_BLOCK == 0
    in_width = w["win"].shape[1]
    c, sa, sb = _rope_tables(pos)
    bpt = tm // MOBA_BLOCK
    row_spec = pl.BlockSpec((tm, ATT_WIDTH), lambda i: (i, 0))
    out_shape = [
        jax.ShapeDtypeStruct((n, ATT_WIDTH), F32),
        jax.ShapeDtypeStruct((n, ATT_WIDTH), F32),
        jax.ShapeDtypeStruct((n, POOL_WIDTH), F32),
        jax.ShapeDtypeStruct((N_HEADS, n, HEAD_DIM), BF16),
        jax.ShapeDtypeStruct((N_HEADS, n, 2 * HEAD_DIM), BF16),
        jax.ShapeDtypeStruct((N_HEADS, n, 2 * HEAD_DIM), BF16),
        jax.ShapeDtypeStruct((n // tm, bpt, ATT_WIDTH), F32),
        jax.ShapeDtypeStruct((n, POOL_WIDTH), BF16),
    ]
    out_specs = [
        row_spec, row_spec, row_spec,
        pl.BlockSpec((N_HEADS, tm, HEAD_DIM), lambda i: (0, i, 0)),
        pl.BlockSpec((N_HEADS, tm, 2 * HEAD_DIM), lambda i: (0, i, 0)),
        pl.BlockSpec((N_HEADS, tm, 2 * HEAD_DIM), lambda i: (0, i, 0)),
        pl.BlockSpec((1, bpt, ATT_WIDTH), lambda i: (i, 0, 0)),
        row_spec,
    ]
    in_specs = _inproj_common_specs(tm, d_model, in_width) + [
        _const_spec(w["wpool"].shape), _const_spec((1, POOL_WIDTH))]
    return pl.pallas_call(
        _inproj_prompt_kernel,
        grid=(n // tm,),
        in_specs=in_specs,
        out_specs=out_specs,
        out_shape=out_shape,
        scratch_shapes=[pltpu.VMEM((POOL_MAX, POOL_WIDTH), F32)],
        compiler_params=pltpu.CompilerParams(dimension_semantics=("arbitrary",), vmem_limit_bytes=VMEM_LIMIT),
        name="inproj_prompt",
    )(x, w["g1"], w["win"], w["qg"], w["kg"], w["ones_bd"], c, sa, sb, w["wpool"], w["pscale"])


def _inproj_sample(x, pos, w):
    n, d_model = x.shape
    tm = min(ROW_TILE, n)
    assert n % tm == 0
    in_width = w["win"].shape[1]
    c, sa, sb = _rope_tables(pos)
    row_spec = pl.BlockSpec((tm, ATT_WIDTH), lambda i: (i, 0))
    return pl.pallas_call(
        _inproj_sample_kernel,
        grid=(n // tm,),
        in_specs=_inproj_common_specs(tm, d_model, in_width),
        out_specs=[row_spec] * 4,
        out_shape=[jax.ShapeDtypeStruct((n, ATT_WIDTH), F32)] * 4,
        compiler_params=pltpu.CompilerParams(dimension_semantics=("arbitrary",), vmem_limit_bytes=VMEM_LIMIT),
        name="inproj_sample",
    )(x, w["g1"], w["win"], w["qg"], w["kg"], w["ones_bd"], c, sa, sb)


def _attn_prompt(qhm, kaug, vhm, kmean):
    n_heads, s, _ = qhm.shape
    tq = MOBA_BLOCK
    nb = s // tq
    assert s % (tq * ATTN_BLOCKS_PER_ITER) == 0 and nb <= HEAD_DIM
    hpb = 2
    kmt = kmean.reshape(nb, n_heads, HEAD_DIM).transpose(1, 2, 0)
    kmt = jnp.pad(kmt, ((0, 0), (0, 0), (0, HEAD_DIM - nb))).astype(BF16)
    return pl.pallas_call(
        _attn_prompt_kernel,
        grid=(n_heads // hpb, nb),
        in_specs=[
            pl.BlockSpec((hpb, tq, HEAD_DIM), lambda g, i: (g, i, 0)),
            pl.BlockSpec((hpb, s, 2 * HEAD_DIM), lambda g, i: (g, 0, 0)),
            pl.BlockSpec((hpb, s, 2 * HEAD_DIM), lambda g, i: (g, 0, 0)),
            pl.BlockSpec((hpb, HEAD_DIM, HEAD_DIM), lambda g, i: (g, 0, 0)),
        ],
        out_specs=pl.BlockSpec((tq, hpb * HEAD_DIM), lambda g, i: (i, g)),
        out_shape=jax.ShapeDtypeStruct((s, n_heads * HEAD_DIM), BF16),
        compiler_params=pltpu.CompilerParams(dimension_semantics=("arbitrary", "arbitrary"),
                                             vmem_limit_bytes=VMEM_LIMIT),
        name="attn_prompt",
    )(qhm, kaug, vhm, kmt)


def _attn_sample(q, k_new, v_new, cache_kt, cache_vt, page_table):
    db, ds_len, width = q.shape
    _, n_heads, head_dim, page = cache_kt.shape
    n_pages = page_table.shape[1]
    n_pg = min(SAMPLE_CHUNK_PAGES, n_pages)
    past = n_pages * page
    assert n_heads * head_dim == width and n_pages % n_pg == 0 and (n_pg * page) % MOBA_BLOCK == 0
    assert past % MOBA_BLOCK == 0 and ds_len <= MOBA_BLOCK
    n_kc = n_pages // n_pg
    n_blocks = past // MOBA_BLOCK
    assert MOBA_TOPK <= n_blocks <= LANES

    q4 = q.reshape(db, ds_len, N_HEADS, HEAD_DIM)
    eye = jnp.eye(N_HEADS, dtype=F32)
    qbd = jnp.einsum("bthd,hg->bhtgd", q4, eye).reshape(db, N_HEADS * ds_len, width).astype(BF16)
    n_rows = N_HEADS * ds_len

    def seq_spec(r):
        return pl.BlockSpec((None, r, width), lambda b, c, pt: (b, 0, 0))

    def k_spec(u):
        return pl.BlockSpec((None, n_heads, head_dim, page),
                            lambda b, c, pt: (pt[b, jnp.minimum(c, n_kc - 1) * n_pg + u], 0, 0, 0))

    def v_spec(u):
        return pl.BlockSpec((None, n_heads, head_dim, page),
                            lambda b, c, pt: (pt[b, jnp.maximum(c - n_kc, 0) * n_pg + u], 0, 0, 0))

    return pl.pallas_call(
        functools.partial(_attn_sample_kernel, n_blocks=n_blocks, n_pg=n_pg),
        grid_spec=pltpu.PrefetchScalarGridSpec(
            num_scalar_prefetch=1,
            grid=(db, 2 * n_kc),
            in_specs=[seq_spec(n_rows), seq_spec(ds_len), seq_spec(ds_len)]
            + [k_spec(u) for u in range(n_pg)] + [v_spec(u) for u in range(n_pg)],
            out_specs=seq_spec(ds_len),
            scratch_shapes=[
                pltpu.VMEM((n_rows, past), F32),
                pltpu.VMEM((n_rows, past), BF16),
                pltpu.VMEM((n_rows, width), F32),
                pltpu.VMEM((n_rows, LANES), F32),
            ],
        ),
        out_shape=jax.ShapeDtypeStruct((db, ds_len, width), BF16),
        compiler_params=pltpu.CompilerParams(dimension_semantics=("arbitrary", "arbitrary"),
                                             vmem_limit_bytes=VMEM_LIMIT),
        name="attn_sample",
    )(page_table, qbd, k_new, v_new, *([cache_kt] * n_pg), *([cache_vt] * n_pg))


def _pool_sample(state, p, w, past):
    db, ds_len, width = p.shape
    buf = jnp.concatenate([state, p], axis=1).transpose(1, 0, 2)
    n_buf = buf.shape[0]
    out = pl.pallas_call(
        functools.partial(_pool_sample_kernel, past=past),
        grid=(1,),
        in_specs=[_const_spec((n_buf, db, width)), _const_spec(w["wpool"].shape), _const_spec((1, POOL_WIDTH))],
        out_specs=_const_spec((ds_len, db, width)),
        out_shape=jax.ShapeDtypeStruct((ds_len, db, width), BF16),
        compiler_params=pltpu.CompilerParams(dimension_semantics=("arbitrary",), vmem_limit_bytes=VMEM_LIMIT),
        name="pool_sample",
    )(buf, w["wpool"], w["pscale"])
    return out.transpose(1, 0, 2)


def _mid(x, attn, pool, w):
    t, d_model = x.shape
    tt = MID_TILE
    assert t % tt == 0
    nk = PEER_HEADS * PEER_HALF
    tok = lambda width: pl.BlockSpec((tt, width), lambda i: (i, 0))
    ah = pl.BlockSpec((PEER_N_KEYS, PEER_HEADS, tt), lambda i: (0, 0, i))
    ha = pl.BlockSpec((PEER_HEADS, PEER_N_KEYS, tt), lambda i: (0, 0, i))
    return pl.pallas_call(
        _mid_kernel,
        grid=(t // tt,),
        in_specs=[
            tok(d_model), tok(ATT_WIDTH), tok(POOL_WIDTH),
            _const_spec((ATT_WIDTH, d_model)), _const_spec((POOL_WIDTH, d_model)),
            _const_spec((1, d_model)),
            _const_spec((2 * nk, d_model)),
            _const_spec((2, nk, nk)),
            _const_spec((nk, nk)),
        ],
        out_specs=[tok(d_model), tok(d_model), ah, ah, ha],
        out_shape=[
            jax.ShapeDtypeStruct((t, d_model), F32),
            jax.ShapeDtypeStruct((t, d_model), BF16),
            jax.ShapeDtypeStruct((PEER_N_KEYS, PEER_HEADS, t), F32),
            jax.ShapeDtypeStruct((PEER_N_KEYS, PEER_HEADS, t), F32),
            jax.ShapeDtypeStruct((PEER_HEADS, PEER_N_KEYS, t), F32),
        ],
        compiler_params=pltpu.CompilerParams(dimension_semantics=("arbitrary",), vmem_limit_bytes=VMEM_LIMIT),
        name="outproj_route",
    )(x, attn, pool, w["woa"], w["wop"], w["g2"], w["wqt"], w["ski"], w["skh"])


def _peer(x2, h2, ethr, e1, e2m, w):
    t, d_model = x2.shape
    n_exp = w["u"].shape[0]
    tt = min(PEER_TOK_TILE, t)
    et = PEER_EXP_TILE
    assert t % tt == 0 and n_exp % et == 0 and et % PEER_N_KEYS == 0 and PEER_N_KEYS % PEER_ROW_CHUNK == 0
    ah = pl.BlockSpec((PEER_N_KEYS, PEER_HEADS, tt), lambda i, e: (0, 0, i))
    ha = pl.BlockSpec((PEER_HEADS, PEER_N_KEYS, tt), lambda i, e: (0, 0, i))
    n_tiles = n_exp // et
    return pl.pallas_call(
        _peer_kernel,
        grid=(t // tt, n_tiles + 2),
        in_specs=[
            pl.BlockSpec((tt, d_model), lambda i, e: (i, 0)),
            pl.BlockSpec((tt, d_model), lambda i, e: (i, 0)),
            ah, ah, ha,
            pl.BlockSpec((et, d_model), lambda i, e: (jnp.minimum(e, n_tiles - 1), 0)),
            pl.BlockSpec((d_model, et), lambda i, e: (0, jnp.clip(e - 2, 0, n_tiles - 1))),
        ],
        out_specs=pl.BlockSpec((tt, d_model), lambda i, e: (i, 0)),
        out_shape=jax.ShapeDtypeStruct((t, d_model), F32),
        scratch_shapes=[pltpu.VMEM((d_model, tt), F32), pltpu.VMEM((2, et, tt), F32),
                        pltpu.VMEM((2, et, tt), BF16)],
        compiler_params=pltpu.CompilerParams(dimension_semantics=("arbitrary", "arbitrary"),
                                             vmem_limit_bytes=VMEM_LIMIT),
        name="peer_dense",
    )(x2, h2, ethr, e1, e2m, w["u"], w["vt"])


def _layer_weights(l, norm1_g, w_in, q_norm_g, k_norm_g, w_pool, pool_scale, w_out, norm2_g,
                   peer_wq, peer_subkeys, peer_u, peer_v):
    d_model = w_in.shape[1]
    eye = jnp.eye(PEER_HEADS, dtype=F32)
    sk = peer_subkeys[l]
    nk = PEER_HEADS * PEER_HALF
    wq = peer_wq[l].reshape(d_model, PEER_HEADS, 2, PEER_HALF).transpose(0, 2, 1, 3).reshape(d_model, 2 * nk)
    return {
        "g1": norm1_g[l][None],
        "win": w_in[l].astype(BF16),
        "qg": jnp.tile(q_norm_g[l], N_HEADS)[None],
        "kg": jnp.tile(k_norm_g[l], N_HEADS)[None],
        "ones_bd": jnp.asarray(np.kron(np.eye(N_HEADS), np.ones((HEAD_DIM, HEAD_DIM))), BF16),
        "wpool": w_pool[l].astype(BF16),
        "pscale": pool_scale[l][None],
        "woa": w_out[l][:ATT_WIDTH].astype(BF16),
        "wop": w_out[l][ATT_WIDTH:].astype(BF16),
        "g2": norm2_g[l][None],
        "wqt": wq.T.astype(BF16),
        "ski": jnp.einsum("hckd,hg->ckhgd", sk, eye).reshape(2, nk, nk).astype(BF16),
        "skh": jnp.einsum("hkd,hg->hkgd", sk[:, 1], eye).reshape(nk, nk).astype(BF16),
        "u": peer_u[l].astype(BF16),
        "vt": peer_v[l].T.astype(BF16),
    }


def kernel(x_prompt, x_sample, cache_k, cache_v, state_pool_rows, page_table, norm1_g, w_in, q_norm_g, k_norm_g,
           w_pool, pool_scale, w_out, norm2_g, peer_wq, peer_subkeys, peer_u, peer_v):
    b, s, d_model = x_prompt.shape
    db, ds_len, _ = x_sample.shape
    depth, n_pool, page = cache_k.shape[:3]
    assert b == 1
    past = page_table.shape[1] * page
    pos_p = jnp.arange(s)
    pos_s = jnp.tile(past + jnp.arange(ds_len), db)
    yp = x_prompt.reshape(s, d_model)
    ys = x_sample.reshape(db * ds_len, d_model)
    outs = [[] for _ in range(6)]
    for l in range(depth):
        w = _layer_weights(l, norm1_g, w_in, q_norm_g, k_norm_g, w_pool, pool_scale, w_out, norm2_g,
                           peer_wq, peer_subkeys, peer_u, peer_v)
        kp, vp, pp, qhm, kaug, vhm, kmean, pool_p = _inproj_prompt(yp, pos_p, w)
        attn_p = _attn_prompt(qhm, kaug, vhm, kmean.reshape(-1, ATT_WIDTH))
        qs, ks, vs, ps = _inproj_sample(ys, pos_s, w)
        ck = cache_k[l].transpose(0, 2, 3, 1)
        cv = cache_v[l].transpose(0, 2, 3, 1)
        r3 = lambda a: a.reshape(db, ds_len, -1)
        attn_s = _attn_sample(r3(qs), r3(ks), r3(vs), ck, cv, page_table)
        pool_s = _pool_sample(state_pool_rows[l], r3(ps), w, past)
        x_all = jnp.concatenate([yp, ys], axis=0)
        attn_all = jnp.concatenate([attn_p, attn_s.reshape(db * ds_len, -1)], axis=0)
        pool_all = jnp.concatenate([pool_p, pool_s.reshape(db * ds_len, -1)], axis=0)
        x2, h2, ethr, e1, e2m = _mid(x_all, attn_all, pool_all, w)
        y_all = _peer(x2, h2, ethr, e1, e2m, w)
        yp, ys = y_all[:s], y_all[s:]
        outs[0].append(kp.reshape(b, s, N_HEADS, HEAD_DIM))
        outs[1].append(vp.reshape(b, s, N_HEADS, HEAD_DIM))
        outs[2].append(pp[s - POOL_BUF:].reshape(b, POOL_BUF, POOL_WIDTH))
        outs[3].append(ks.reshape(db, ds_len, N_HEADS, HEAD_DIM))
        outs[4].append(vs.reshape(db, ds_len, N_HEADS, HEAD_DIM))
        buf = jnp.concatenate([state_pool_rows[l], r3(ps)], axis=1)
        outs[5].append(buf[:, buf.shape[1] - POOL_BUF:])
    return (yp.reshape(b, s, d_model), ys.reshape(db, ds_len, d_model),
            *[jnp.stack(o) for o in outs])
```
